```python
import jax, jax.numpy as jnp
from jax import lax
import numpy as np

D_MODEL = 2048
BATCH = 4
SEQ = 2048
DEPTH = 2
DEC_BATCH = 128
DEC_SEQ = 8
PAST_LEN = 16384
PAGE_SIZE = 128

W_A = D_MODEL // 4
N_HEADS_A = 4
CONV_K = 31
W_B = D_MODEL // 4
N_HEADS_B = 4
HEAD_B = W_B // N_HEADS_B
CHUNK = 128
W_C = D_MODEL // 2
SSM_GROUP = 16
N_GROUPS_C = W_C // SSM_GROUP
SSM_STATE = 64
W_MIX = W_A + W_B + W_C
D_IN = 2 * W_A + 2 * W_B + W_C
D_FF = -(-(8 * D_MODEL) // (3 * 256)) * 256
EPS = 1e-6
LOG_DT_MIN = -6.907755
LOG_DT_MAX = -2.302585

kernel_name = "hymba_conv_gmlp_s5_decode_step"


def rmsnorm(x, g):
    x32 = x.astype(jnp.float32)
    y = x32 * lax.rsqrt(jnp.mean(x32 * x32, axis=-1, keepdims=True) + EPS)
    return (y * g.astype(jnp.float32)).astype(x.dtype)


def layernorm_heads(x, g, b, n_heads):
    n, t, c = x.shape
    x32 = x.astype(jnp.float32).reshape(n, t, n_heads, c // n_heads)
    mu = jnp.mean(x32, axis=-1, keepdims=True)
    xc = x32 - mu
    var = jnp.mean(xc * xc, axis=-1, keepdims=True)
    y = (xc * lax.rsqrt(var + EPS)).reshape(n, t, c)
    return (y * g.astype(jnp.float32) + b.astype(jnp.float32)).astype(x.dtype)


def conformer_conv(a_val, a_gate, conv_prev, conv_w, conv_b, ln_g, ln_b):
    z = a_val * jax.nn.sigmoid(a_gate)
    zp = jnp.concatenate([conv_prev.astype(z.dtype), z], axis=1)
    y = lax.conv_general_dilated(zp, conv_w[:, None, :].astype(z.dtype), window_strides=(1,), padding="VALID",
                                 dimension_numbers=("NWC", "WIO", "NWC"), feature_group_count=W_A)
    y = layernorm_heads(y + conv_b, ln_g, ln_b, N_HEADS_A)
    return jax.nn.silu(y), zp[:, -(CONV_K - 1):]


def chunk_spatial_gate(u, v, ln_g, ln_b, w_s, b_s):
    n, t, _ = v.shape
    vn = layernorm_heads(v, ln_g, ln_b, 1)
    n_chunks = -(-t // CHUNK)
    pad = n_chunks * CHUNK - t
    vp = jnp.pad(vn, ((0, 0), (0, pad), (0, 0))).reshape(n, n_chunks, CHUNK, N_HEADS_B, HEAD_B)
    mask = jnp.tril(jnp.ones((CHUNK, CHUNK), dtype=bool))
    w = jnp.where(mask, w_s, jnp.zeros_like(w_s))
    mixed = jnp.einsum("hts,ncshd->ncthd", w, vp) + jnp.transpose(b_s)[:, :, None]
    mixed = mixed.reshape(n, n_chunks * CHUNK, W_B)[:, :t]
    return u * mixed, vn


def s5_branch(u, h0_re, h0_im, a_re, a_im, log_dt, b_re, b_im, c_re, c_im, d_skip, w_glu, b_glu):
    f32 = jnp.float32
    n, t, _ = u.shape
    ug = u.astype(f32).reshape(n, t, N_GROUPS_C, SSM_GROUP)
    dt = jnp.exp(log_dt.astype(f32))[:, None]
    ar = a_re.astype(f32)
    ai = a_im.astype(f32)
    mag = jnp.exp(dt * ar)
    abar_re = mag * jnp.cos(dt * ai)
    abar_im = mag * jnp.sin(dt * ai)
    p_re = abar_re - 1.0
    p_im = abar_im
    den = ar * ar + ai * ai
    k_re = (p_re * ar + p_im * ai) / den
    k_im = (p_im * ar - p_re * ai) / den
    br = b_re.astype(f32)
    bi = b_im.astype(f32)
    bbar_re = k_re[..., None] * br - k_im[..., None] * bi
    bbar_im = k_re[..., None] * bi + k_im[..., None] * br
    bu_re = jnp.einsum("ntgc,gpc->ntgp", ug, bbar_re)
    bu_im = jnp.einsum("ntgc,gpc->ntgp", ug, bbar_im)
    h0r = h0_re.astype(f32)
    h0i = h0_im.astype(f32)
    bu_re = bu_re.at[:, 0].add(abar_re * h0r - abar_im * h0i)
    bu_im = bu_im.at[:, 0].add(abar_re * h0i + abar_im * h0r)
    a_re_t = jnp.broadcast_to(abar_re, bu_re.shape)
    a_im_t = jnp.broadcast_to(abar_im, bu_im.shape)

    def combine(e1, e2):
        a1r, a1i, b1r, b1i = e1
        a2r, a2i, b2r, b2i = e2
        return (a1r * a2r - a1i * a2i,
                a1r * a2i + a1i * a2r,
                a2r * b1r - a2i * b1i + b2r,
                a2r * b1i + a2i * b1r + b2i)

    _, _, hr, hi = lax.associative_scan(combine, (a_re_t, a_im_t, bu_re, bu_im), axis=1)
    y = (jnp.einsum("ntgp,gcp->ntgc", hr, c_re.astype(f32))
         - jnp.einsum("ntgp,gcp->ntgc", hi, c_im.astype(f32))
         + d_skip.astype(f32) * ug)
    z = jax.nn.gelu(y)
    gl = jnp.einsum("ntgc,gck->ntgk", z, w_glu.astype(f32)) + b_glu.astype(f32)
    out = gl[..., :SSM_GROUP] * jax.nn.sigmoid(gl[..., SSM_GROUP:])
    return (out.reshape(n, t, W_C).astype(u.dtype),
            hr[:, -1].astype(h0_re.dtype),
            hi[:, -1].astype(h0_im.dtype))


def hybrid_layer(x, conv_prev, h0_re, h0_im, prm):
    h = rmsnorm(x, prm["g_mix"])
    proj = h @ prm["w_in"]
    a_val, a_gate, b_u, b_v, c_u = jnp.split(
        proj, [W_A, 2 * W_A, 2 * W_A + W_B, 2 * W_A + 2 * W_B], axis=-1)
    out_a, conv_new = conformer_conv(a_val, a_gate, conv_prev, prm["conv_w"], prm["conv_b"],
                                     prm["ln_a_g"], prm["ln_a_b"])
    out_b, v_rows = chunk_spatial_gate(b_u, b_v, prm["ln_v_g"], prm["ln_v_b"], prm["w_s"], prm["b_s"])
    out_c, hr, hi = s5_branch(c_u, h0_re, h0_im, prm["a_re"], prm["a_im"], prm["log_dt"],
                              prm["b_re"], prm["b_im"], prm["c_re"], prm["c_im"],
                              prm["d_skip"], prm["w_glu"], prm["b_glu"])
    x = x + jnp.concatenate([out_a, out_b, out_c], axis=-1) @ prm["w_out"]
    h2 = rmsnorm(x, prm["g_ffn"])
    gate, up = jnp.split(h2 @ prm["w_gu"], 2, axis=-1)
    x = x + (jax.nn.silu(gate) * up) @ prm["w_down"]
    return x, conv_new, hr, hi, v_rows


def setup_inputs(seed: int = 0) -> dict:
    key = jax.random.key(seed)
    ks = jax.random.split(key, 32)
    nrm = jax.random.normal
    f32 = jnp.float32
    inp = {}
    inp["x_prompt"] = nrm(ks[0], (BATCH, SEQ, D_MODEL), f32)
    inp["x_sample"] = nrm(ks[1], (DEC_BATCH, DEC_SEQ, D_MODEL), f32)
    inp["state_conv"] = 0.5 * nrm(ks[2], (DEPTH, DEC_BATCH, CONV_K - 1, W_A), f32)
    inp["state_ssm_re"] = 0.05 * nrm(ks[3], (DEPTH, DEC_BATCH, N_GROUPS_C, SSM_STATE), f32)
    inp["state_ssm_im"] = 0.05 * nrm(ks[4], (DEPTH, DEC_BATCH, N_GROUPS_C, SSM_STATE), f32)
    inp["g_mix"] = 1.0 + 0.02 * nrm(ks[5], (DEPTH, D_MODEL), f32)
    inp["w_in"] = nrm(ks[6], (DEPTH, D_MODEL, D_IN), f32) * D_MODEL ** -0.5
    inp["conv_w"] = nrm(ks[7], (DEPTH, CONV_K, W_A), f32) * CONV_K ** -0.5
    inp["conv_b"] = 0.02 * nrm(ks[8], (DEPTH, W_A), f32)
    inp["ln_a_g"] = 1.0 + 0.02 * nrm(ks[9], (DEPTH, W_A), f32)
    inp["ln_a_b"] = 0.02 * nrm(ks[10], (DEPTH, W_A), f32)
    inp["ln_v_g"] = 1.0 + 0.02 * nrm(ks[11], (DEPTH, W_B), f32)
    inp["ln_v_b"] = 0.02 * nrm(ks[12], (DEPTH, W_B), f32)
    inp["w_s"] = nrm(ks[13], (DEPTH, N_HEADS_B, CHUNK, CHUNK), f32) * CHUNK ** -0.5
    inp["b_s"] = 1.0 + 0.02 * nrm(ks[14], (DEPTH, N_HEADS_B, CHUNK), f32)
    inp["a_re"] = -0.5 + 0.01 * nrm(ks[15], (DEPTH, N_GROUPS_C, SSM_STATE), f32)
    inp["a_im"] = jnp.tile(jnp.pi * jnp.arange(SSM_STATE, dtype=f32), (DEPTH, N_GROUPS_C, 1))
    inp["log_dt"] = jax.random.uniform(ks[16], (DEPTH, N_GROUPS_C), f32, LOG_DT_MIN, LOG_DT_MAX)
    inp["b_re"] = nrm(ks[17], (DEPTH, N_GROUPS_C, SSM_STATE, SSM_GROUP), f32) * (2 * SSM_GROUP) ** -0.5
    inp["b_im"] = nrm(ks[18], (DEPTH, N_GROUPS_C, SSM_STATE, SSM_GROUP), f32) * (2 * SSM_GROUP) ** -0.5
    inp["c_re"] = nrm(ks[19], (DEPTH, N_GROUPS_C, SSM_GROUP, SSM_STATE), f32) * SSM_STATE ** -0.5
    inp["c_im"] = nrm(ks[20], (DEPTH, N_GROUPS_C, SSM_GROUP, SSM_STATE), f32) * SSM_STATE ** -0.5
    inp["d_skip"] = nrm(ks[21], (DEPTH, N_GROUPS_C, SSM_GROUP), f32)
    inp["w_glu"] = nrm(ks[22], (DEPTH, N_GROUPS_C, SSM_GROUP, 2 * SSM_GROUP), f32) * SSM_GROUP ** -0.5
    inp["b_glu"] = 0.02 * nrm(ks[23], (DEPTH, N_GROUPS_C, 2 * SSM_GROUP), f32)
    inp["w_out"] = nrm(ks[24], (DEPTH, W_MIX, D_MODEL), f32) * W_MIX ** -0.5
    inp["g_ffn"] = 1.0 + 0.02 * nrm(ks[25], (DEPTH, D_MODEL), f32)
    inp["w_gu"] = nrm(ks[26], (DEPTH, D_MODEL, 2 * D_FF), f32) * D_MODEL ** -0.5
    inp["w_down"] = nrm(ks[27], (DEPTH, D_FF, D_MODEL), f32) * D_FF ** -0.5
    inp["g_final"] = 1.0 + 0.02 * nrm(ks[28], (D_MODEL,), f32)
    return inp


def reference(x_prompt, x_sample, state_conv, state_ssm_re, state_ssm_im, g_mix, w_in, conv_w, conv_b,
              ln_a_g, ln_a_b, ln_v_g, ln_v_b, w_s, b_s, a_re, a_im, log_dt, b_re, b_im, c_re, c_im,
              d_skip, w_glu, b_glu, w_out, g_ffn, w_gu, w_down, g_final):
    xp = x_prompt
    xs = x_sample
    n_p = x_prompt.shape[0]
    conv_p, ssr_p, ssi_p = [], [], []
    conv_s, ssr_s, ssi_s, v_s = [], [], [], []
    for l in range(DEPTH):
        prm = {"g_mix": g_mix[l], "w_in": w_in[l], "conv_w": conv_w[l], "conv_b": conv_b[l],
               "ln_a_g": ln_a_g[l], "ln_a_b": ln_a_b[l], "ln_v_g": ln_v_g[l], "ln_v_b": ln_v_b[l],
               "w_s": w_s[l], "b_s": b_s[l], "a_re": a_re[l], "a_im": a_im[l], "log_dt": log_dt[l],
               "b_re": b_re[l], "b_im": b_im[l], "c_re": c_re[l], "c_im": c_im[l], "d_skip": d_skip[l],
               "w_glu": w_glu[l], "b_glu": b_glu[l], "w_out": w_out[l], "g_ffn": g_ffn[l],
               "w_gu": w_gu[l], "w_down": w_down[l]}
        zero_conv = jnp.zeros((n_p, CONV_K - 1, W_A), xp.dtype)
        zero_ssm = jnp.zeros((n_p, N_GROUPS_C, SSM_STATE), state_ssm_re.dtype)
        xp, cp, hrp, hip, _ = hybrid_layer(xp, zero_conv, zero_ssm, zero_ssm, prm)
        conv_p.append(cp)
        ssr_p.append(hrp)
        ssi_p.append(hip)
        xs, cs, hrs, his, vs = hybrid_layer(xs, state_conv[l], state_ssm_re[l], state_ssm_im[l], prm)
        conv_s.append(cs)
        ssr_s.append(hrs)
        ssi_s.append(his)
        v_s.append(vs)
    y_prompt = rmsnorm(xp, g_final)
    y_sample = rmsnorm(xs, g_final)
    new_conv_prompt = jnp.stack(conv_p)
    new_ssm_re_prompt = jnp.stack(ssr_p)
    new_ssm_im_prompt = jnp.stack(ssi_p)
    new_conv_sample = jnp.stack(conv_s)
    new_ssm_re_sample = jnp.stack(ssr_s)
    new_ssm_im_sample = jnp.stack(ssi_s)
    new_chunk_v_sample = jnp.stack(v_s)
    return (y_prompt, y_sample, new_conv_prompt, new_ssm_re_prompt, new_ssm_im_prompt,
            new_conv_sample, new_ssm_re_sample, new_ssm_im_sample, new_chunk_v_sample)
```

```python
import functools

import jax
import jax.numpy as jnp
from jax import lax
from jax.experimental import pallas as pl
from jax.experimental.pallas import tpu as pltpu

F32 = jnp.float32
BF16 = jnp.bfloat16

D_MODEL = 2048
N_PROMPT = 4
SEQ = 2048
DEPTH = 2
N_SAMPLE = 128
DEC_SEQ = 8
ROWS_P = N_PROMPT * SEQ
ROWS_S = N_SAMPLE * DEC_SEQ
ROWS = ROWS_P + ROWS_S

W_A = 512
N_HEADS_A = 4
CONV_K = 31
W_B = 512
N_HEADS_B = 4
CHUNK = 128
W_C = 1024
SSM_GROUP = 16
N_GROUPS_C = 64
SSM_STATE = 64
W_MIX = W_A + W_B + W_C
D_IN = 2 * W_A + 2 * W_B + W_C
D_FF = 5632
EPS = 1e-6

LANES = 128
SUBLANES = 8
GROUPS_PER_BLOCK = LANES // SSM_GROUP
N_CBLOCKS = W_C // LANES
STATE_BLOCK = GROUPS_PER_BLOCK * SSM_STATE
VMEM_LIMIT = 56 * 1024 * 1024


def _params(semantics, vmem=VMEM_LIMIT):
    return pltpu.CompilerParams(dimension_semantics=semantics, vmem_limit_bytes=vmem)


def _rmsnorm_rows_to(x_ref, g_ref, h_ref, rows, chunk=128):
    def body(r, carry):
        s = pl.multiple_of(r * chunk, chunk)
        x = x_ref[pl.ds(s, chunk), :]
        ms = jnp.mean(x * x, axis=-1, keepdims=True)
        h_ref[pl.ds(s, chunk), :] = (x * lax.rsqrt(ms + EPS) * g_ref[...]).astype(BF16)
        return carry
    lax.fori_loop(0, rows // chunk, body, 0)


def _in_proj_kernel(x_ref, g_ref, w_ref, o_ref, h_ref, *, bm):
    @pl.when(pl.program_id(1) == 0)
    def _():
        _rmsnorm_rows_to(x_ref, g_ref, h_ref, bm)
    o_ref[...] = jnp.dot(h_ref[...], w_ref[...], preferred_element_type=F32)


def in_proj(x, g, w, *, bm=1024, bn=1024):
    m, k = x.shape
    n = w.shape[1]
    return pl.pallas_call(
        functools.partial(_in_proj_kernel, bm=bm),
        grid=(m // bm, n // bn),
        in_specs=[pl.BlockSpec((bm, k), lambda i, j: (i, 0)),
                  pl.BlockSpec((1, k), lambda i, j: (0, 0)),
                  pl.BlockSpec((k, bn), lambda i, j: (0, j))],
        out_specs=pl.BlockSpec((bm, bn), lambda i, j: (i, j)),
        out_shape=jax.ShapeDtypeStruct((m, n), F32),
        scratch_shapes=[pltpu.VMEM((bm, k), BF16)],
        compiler_params=_params(("parallel", "arbitrary")),
    )(x, g.reshape(1, k), w)


def _head_layernorm_silu(y, lg_ref, lb_ref, o_ref, row0, nrows):
    for h in range(N_HEADS_A):
        cols = slice(h * LANES, (h + 1) * LANES)
        yh = y[:, cols]
        mu = jnp.mean(yh, axis=-1, keepdims=True)
        xc = yh - mu
        var = jnp.mean(xc * xc, axis=-1, keepdims=True)
        t = xc * lax.rsqrt(var + EPS) * lg_ref[:, cols] + lb_ref[:, cols]
        o_ref[row0:row0 + nrows, cols] = (t * jax.nn.sigmoid(t)).astype(BF16)


HALO = 32
CONV_ROWS = 32


def _conv_prompt_kernel(av_ref, ag_ref, cw_ref, cb_ref, lg_ref, lb_ref, o_ref, st_ref, zp_ref, *, tb):
    t = pl.program_id(1)

    @pl.when(t == 0)
    def _():
        zp_ref[0:HALO, :] = jnp.zeros((HALO, W_A), F32)

    @pl.when(t > 0)
    def _():
        zp_ref[0:HALO, :] = zp_ref[tb:tb + HALO, :]

    zp_ref[HALO:HALO + tb, :] = av_ref[...] * jax.nn.sigmoid(ag_ref[...])

    off = HALO - (CONV_K - 1)
    for r in range(tb // CONV_ROWS):
        base = r * CONV_ROWS + off
        acc = jnp.zeros((CONV_ROWS, W_A), F32)
        for k in range(CONV_K):
            acc = acc + cw_ref[k:k + 1, :] * zp_ref[base + k:base + k + CONV_ROWS, :]
        _head_layernorm_silu(acc + cb_ref[...], lg_ref, lb_ref, o_ref, r * CONV_ROWS, CONV_ROWS)

    @pl.when(t == pl.num_programs(1) - 1)
    def _():
        st_ref[0] = zp_ref[HALO + tb - (CONV_K - 1):HALO + tb, :]


def conv_prompt(proj, cw, cb, lg, lb, *, tb=256):
    nt = SEQ // tb
    return pl.pallas_call(
        functools.partial(_conv_prompt_kernel, tb=tb),
        grid=(N_PROMPT, nt),
        in_specs=[pl.BlockSpec((tb, W_A), lambda s, t: (s * nt + t, 0)),
                  pl.BlockSpec((tb, W_A), lambda s, t: (s * nt + t, 1)),
                  pl.BlockSpec((CONV_K, W_A), lambda s, t: (0, 0)),
                  pl.BlockSpec((1, W_A), lambda s, t: (0, 0)),
                  pl.BlockSpec((1, W_A), lambda s, t: (0, 0)),
                  pl.BlockSpec((1, W_A), lambda s, t: (0, 0))],
        out_specs=[pl.BlockSpec((tb, W_A), lambda s, t: (s * nt + t, 0)),
                   pl.BlockSpec((1, CONV_K - 1, W_A), lambda s, t: (s, 0, 0))],
        out_shape=[jax.ShapeDtypeStruct((ROWS, W_A), BF16),
                   jax.ShapeDtypeStruct((N_PROMPT, CONV_K - 1, W_A), F32)],
        scratch_shapes=[pltpu.VMEM((HALO + tb, W_A), F32)],
        compiler_params=_params(("parallel", "arbitrary")),
    )(proj, proj, cw, cb.reshape(1, W_A), lg.reshape(1, W_A), lb.reshape(1, W_A))


SEQ_SLOT = 40


def _conv_sample_kernel(av_ref, ag_ref, hist_ref, cw_ref, cb_ref, lg_ref, lb_ref, buf_ref,
                        o_ref, st_ref, zp_ref, y_ref, *, sb):
    del buf_ref
    hist_rows = CONV_K - 1
    z = av_ref[...] * jax.nn.sigmoid(ag_ref[...])
    for s in range(sb):
        b0 = s * SEQ_SLOT
        zp_ref[b0:b0 + hist_rows, :] = hist_ref[s]
        zp_ref[b0 + hist_rows:b0 + hist_rows + DEC_SEQ, :] = z[s * DEC_SEQ:(s + 1) * DEC_SEQ, :]
    for s in range(sb):
        b0 = s * SEQ_SLOT
        acc = jnp.zeros((DEC_SEQ, W_A), F32)
        for k in range(CONV_K):
            acc = acc + cw_ref[k:k + 1, :] * zp_ref[b0 + k:b0 + k + DEC_SEQ, :]
        y_ref[s * DEC_SEQ:(s + 1) * DEC_SEQ, :] = acc + cb_ref[...]
        st_ref[s] = zp_ref[b0 + DEC_SEQ:b0 + DEC_SEQ + hist_rows, :]
    _head_layernorm_silu(y_ref[...], lg_ref, lb_ref, o_ref, 0, sb * DEC_SEQ)


def conv_sample(proj, hist, cw, cb, lg, lb, out_buf, *, sb=16):
    rb = sb * DEC_SEQ
    row_blk0 = ROWS_P // rb
    return pl.pallas_call(
        functools.partial(_conv_sample_kernel, sb=sb),
        grid=(N_SAMPLE // sb,),
        in_specs=[pl.BlockSpec((rb, W_A), lambda i: (row_blk0 + i, 0)),
                  pl.BlockSpec((rb, W_A), lambda i: (row_blk0 + i, 1)),
                  pl.BlockSpec((sb, CONV_K - 1, W_A), lambda i: (i, 0, 0)),
                  pl.BlockSpec((CONV_K, W_A), lambda i: (0, 0)),
                  pl.BlockSpec((1, W_A), lambda i: (0, 0)),
                  pl.BlockSpec((1, W_A), lambda i: (0, 0)),
                  pl.BlockSpec((1, W_A), lambda i: (0, 0)),
                  pl.BlockSpec(memory_space=pl.ANY)],
        out_specs=[pl.BlockSpec((rb, W_A), lambda i: (row_blk0 + i, 0)),
                   pl.BlockSpec((sb, CONV_K - 1, W_A), lambda i: (i, 0, 0))],
        out_shape=[jax.ShapeDtypeStruct((ROWS, W_A), BF16),
                   jax.ShapeDtypeStruct((N_SAMPLE, CONV_K - 1, W_A), F32)],
        scratch_shapes=[pltpu.VMEM((sb * SEQ_SLOT, W_A), F32), pltpu.VMEM((rb, W_A), F32)],
        input_output_aliases={7: 0},
        compiler_params=_params(("parallel",)),
    )(proj, proj, hist, cw, cb.reshape(1, W_A), lg.reshape(1, W_A), lb.reshape(1, W_A), out_buf)


def _gate_kernel(u_ref, v_ref, g_ref, b_ref, w_ref, bias_ref, *rest, cpb, blk, emit_vn, aliased):
    rest = list(rest)
    if aliased:
        rest.pop(0)
    o_ref = rest.pop(0)
    vn_ref = rest.pop(0) if emit_vn else None

    row = lax.broadcasted_iota(jnp.int32, (CHUNK, CHUNK), 0)
    col = lax.broadcasted_iota(jnp.int32, (CHUNK, CHUNK), 1)
    lo = blk - 1
    mask = ((row & ~lo) == (col & ~lo)) & ((col & lo) <= (row & lo))
    w_heads = [jnp.where(mask, w_ref[h], 0.0).astype(BF16) for h in range(N_HEADS_B)]

    for c in range(cpb):
        rows = slice(c * CHUNK, (c + 1) * CHUNK)
        v = v_ref[rows, :]
        mu = jnp.mean(v, axis=-1, keepdims=True)
        xc = v - mu
        var = jnp.mean(xc * xc, axis=-1, keepdims=True)
        vn = xc * lax.rsqrt(var + EPS) * g_ref[...] + b_ref[...]
        if emit_vn:
            vn_ref[rows, :] = vn
        for h in range(N_HEADS_B):
            cols = slice(h * LANES, (h + 1) * LANES)
            mixed = jnp.dot(w_heads[h], vn[:, cols].astype(BF16), preferred_element_type=F32)
            o_ref[rows, cols] = (u_ref[rows, cols] * (mixed + bias_ref[:, cols])).astype(BF16)


def spatial_gate(proj, g, b, w_heads, bias_full, *, row0, nrows, cpb, blk, out_buf=None):
    rb = cpb * CHUNK
    blk0 = row0 // rb
    aliased = out_buf is not None
    emit_vn = aliased
    in_specs = [pl.BlockSpec((rb, W_B), lambda i: (blk0 + i, 2)),
                pl.BlockSpec((rb, W_B), lambda i: (blk0 + i, 3)),
                pl.BlockSpec((1, W_B), lambda i: (0, 0)),
                pl.BlockSpec((1, W_B), lambda i: (0, 0)),
                pl.BlockSpec((N_HEADS_B, CHUNK, CHUNK), lambda i: (0, 0, 0)),
                pl.BlockSpec((CHUNK, W_B), lambda i: (0, 0))]
    args = [proj, proj, g.reshape(1, W_B), b.reshape(1, W_B), w_heads, bias_full]
    out_specs = [pl.BlockSpec((rb, W_B), lambda i: (blk0 + i, 0))]
    out_shape = [jax.ShapeDtypeStruct((ROWS, W_B), BF16)]
    aliases = {}
    if aliased:
        in_specs.append(pl.BlockSpec(memory_space=pl.ANY))
        args.append(out_buf)
        aliases = {6: 0}
    if emit_vn:
        out_specs.append(pl.BlockSpec((rb, W_B), lambda i: (i, 0)))
        out_shape.append(jax.ShapeDtypeStruct((nrows, W_B), F32))
    return pl.pallas_call(
        functools.partial(_gate_kernel, cpb=cpb, blk=blk, emit_vn=emit_vn, aliased=aliased),
        grid=(nrows // rb,),
        in_specs=in_specs, out_specs=out_specs, out_shape=out_shape,
        input_output_aliases=aliases,
        compiler_params=_params(("parallel",)),
    )(*args)


def _scan_tile(xr, xi, tab_ref, cr, ci):
    for n, k in enumerate((1, 2, 4)):
        ar = tab_ref[0, 2 * n]
        ai = tab_ref[0, 2 * n + 1]
        sr = pltpu.roll(xr, k, 0)
        si = pltpu.roll(xi, k, 0)
        xr, xi = xr + (ar * sr - ai * si), xi + (ar * si + ai * sr)
    pr = tab_ref[0, 6]
    pi = tab_ref[0, 7]
    hr = xr + (pr * cr - pi * ci)
    hi = xi + (pr * ci + pi * cr)
    return hr, hi


def _last_row(x):
    return jnp.broadcast_to(x[SUBLANES - 1:SUBLANES, :], x.shape)


def _ssm_output(u, buf_ref, cbd_ref, d_ref, wg_ref, bg_ref, o_ref):
    y = jnp.dot(buf_ref[...].astype(BF16), cbd_ref[0], preferred_element_type=F32) + d_ref[0] * u
    z = jax.nn.gelu(y)
    gl = jnp.dot(z.astype(BF16), wg_ref[0], preferred_element_type=F32) + bg_ref[0]
    o_ref[...] = (gl[:, :LANES] * jax.nn.sigmoid(gl[:, LANES:])).astype(BF16)


def _ssm_prompt_kernel(u_ref, bbd_ref, tab_ref, cbd_ref, d_ref, wg_ref, bg_ref,
                       o_ref, hr_ref, hi_ref, buf_ref, cr_ref, ci_ref, *, tb):
    t = pl.program_id(2)

    @pl.when(t == 0)
    def _():
        cr_ref[...] = jnp.zeros_like(cr_ref)
        ci_ref[...] = jnp.zeros_like(ci_ref)

    u = u_ref[...]
    buf_ref[...] = jnp.dot(u.astype(BF16), bbd_ref[0], preferred_element_type=F32)

    def tile(i, carry):
        cr, ci = carry
        s = pl.multiple_of(i * SUBLANES, SUBLANES)
        xr = buf_ref[pl.ds(s, SUBLANES), 0:STATE_BLOCK]
        xi = buf_ref[pl.ds(s, SUBLANES), STATE_BLOCK:2 * STATE_BLOCK]
        hr, hi = _scan_tile(xr, xi, tab_ref, cr, ci)
        buf_ref[pl.ds(s, SUBLANES), 0:STATE_BLOCK] = hr
        buf_ref[pl.ds(s, SUBLANES), STATE_BLOCK:2 * STATE_BLOCK] = hi
        return _last_row(hr), _last_row(hi)

    cr, ci = lax.fori_loop(0, tb // SUBLANES, tile, (cr_ref[...], ci_ref[...]))
    cr_ref[...] = cr
    ci_ref[...] = ci

    @pl.when(t == pl.num_programs(2) - 1)
    def _():
        hr_ref[0, 0] = cr[0:1, :]
        hi_ref[0, 0] = ci[0:1, :]

    _ssm_output(u, buf_ref, cbd_ref, d_ref, wg_ref, bg_ref, o_ref)


def _ssm_weight_specs(block_axis):
    def at(*tail):
        def index_map(*ids):
            return (ids[block_axis],) + tail
        return index_map
    return [pl.BlockSpec((1, LANES, 2 * STATE_BLOCK), at(0, 0)),
            pl.BlockSpec((1, 8, SUBLANES, STATE_BLOCK), at(0, 0, 0)),
            pl.BlockSpec((1, 2 * STATE_BLOCK, LANES), at(0, 0)),
            pl.BlockSpec((1, 1, LANES), at(0, 0)),
            pl.BlockSpec((1, LANES, 2 * LANES), at(0, 0)),
            pl.BlockSpec((1, 1, 2 * LANES), at(0, 0))]


def ssm_prompt(proj, wts, *, tb=512):
    nt = SEQ // tb
    cu_blk0 = (2 * W_A + 2 * W_B) // LANES
    state_shape = jax.ShapeDtypeStruct((N_PROMPT, N_CBLOCKS, 1, STATE_BLOCK), F32)
    state_spec = pl.BlockSpec((1, 1, 1, STATE_BLOCK), lambda s, b, t: (s, b, 0, 0))
    return pl.pallas_call(
        functools.partial(_ssm_prompt_kernel, tb=tb),
        grid=(N_PROMPT, N_CBLOCKS, nt),
        in_specs=[pl.BlockSpec((tb, LANES), lambda s, b, t: (s * nt + t, cu_blk0 + b))]
        + _ssm_weight_specs(1),
        out_specs=[pl.BlockSpec((tb, LANES), lambda s, b, t: (s * nt + t, b)), state_spec, state_spec],
        out_shape=[jax.ShapeDtypeStruct((ROWS, W_C), BF16), state_shape, state_shape],
        scratch_shapes=[pltpu.VMEM((tb, 2 * STATE_BLOCK), F32),
                        pltpu.VMEM((SUBLANES, STATE_BLOCK), F32),
                        pltpu.VMEM((SUBLANES, STATE_BLOCK), F32)],
        compiler_params=_params(("parallel", "parallel", "arbitrary")),
    )(proj, *wts)


def _ssm_sample_kernel(u_ref, bbd_ref, tab_ref, cbd_ref, d_ref, wg_ref, bg_ref, h0r_ref, h0i_ref,
                       buf_in_ref, o_ref, hr_ref, hi_ref, buf_ref, *, sb):
    del buf_in_ref
    u = u_ref[...]
    buf_ref[...] = jnp.dot(u.astype(BF16), bbd_ref[0], preferred_element_type=F32)

    def tile(i, carry):
        s = pl.multiple_of(i * SUBLANES, SUBLANES)
        cr = jnp.broadcast_to(h0r_ref[pl.ds(i, 1), :], (SUBLANES, STATE_BLOCK))
        ci = jnp.broadcast_to(h0i_ref[pl.ds(i, 1), :], (SUBLANES, STATE_BLOCK))
        xr = buf_ref[pl.ds(s, SUBLANES), 0:STATE_BLOCK]
        xi = buf_ref[pl.ds(s, SUBLANES), STATE_BLOCK:2 * STATE_BLOCK]
        hr, hi = _scan_tile(xr, xi, tab_ref, cr, ci)
        buf_ref[pl.ds(s, SUBLANES), 0:STATE_BLOCK] = hr
        buf_ref[pl.ds(s, SUBLANES), STATE_BLOCK:2 * STATE_BLOCK] = hi
        hr_ref[pl.ds(i, 1), :] = hr[SUBLANES - 1:SUBLANES, :]
        hi_ref[pl.ds(i, 1), :] = hi[SUBLANES - 1:SUBLANES, :]
        return carry

    lax.fori_loop(0, sb, tile, 0)
    _ssm_output(u, buf_ref, cbd_ref, d_ref, wg_ref, bg_ref, o_ref)


def ssm_sample(proj, wts, h0r, h0i, out_buf, *, sb=16):
    rb = sb * DEC_SEQ
    row_blk0 = ROWS_P // rb
    cu_blk0 = (2 * W_A + 2 * W_B) // LANES
    state_shape = jax.ShapeDtypeStruct((N_SAMPLE, N_CBLOCKS * STATE_BLOCK), F32)
    state_spec = pl.BlockSpec((sb, STATE_BLOCK), lambda i, b: (i, b))
    return pl.pallas_call(
        functools.partial(_ssm_sample_kernel, sb=sb),
        grid=(N_SAMPLE // sb, N_CBLOCKS),
        in_specs=[pl.BlockSpec((rb, LANES), lambda i, b: (row_blk0 + i, cu_blk0 + b))]
        + _ssm_weight_specs(1)
        + [state_spec, state_spec, pl.BlockSpec(memory_space=pl.ANY)],
        out_specs=[pl.BlockSpec((rb, LANES), lambda i, b: (row_blk0 + i, b)), state_spec, state_spec],
        out_shape=[jax.ShapeDtypeStruct((ROWS, W_C), BF16), state_shape, state_shape],
        scratch_shapes=[pltpu.VMEM((rb, 2 * STATE_BLOCK), F32)],
        input_output_aliases={9: 0},
        compiler_params=_params(("parallel", "parallel")),
    )(proj, *wts, h0r, h0i, out_buf)


def _blockdiag(w):
    nb, ng, r, c = w.shape
    eye = jnp.eye(ng, dtype=w.dtype)
    return jnp.einsum("bgrc,gh->bgrhc", w, eye).reshape(nb, ng * r, ng * c)


def ssm_weights(a_re, a_im, log_dt, b_re, b_im, c_re, c_im, d_skip, w_glu, b_glu):
    dt = jnp.exp(log_dt)[:, None]
    mag = jnp.exp(dt * a_re)
    abar_re = mag * jnp.cos(dt * a_im)
    abar_im = mag * jnp.sin(dt * a_im)
    p_re = abar_re - 1.0
    p_im = abar_im
    den = a_re * a_re + a_im * a_im
    k_re = (p_re * a_re + p_im * a_im) / den
    k_im = (p_im * a_re - p_re * a_im) / den
    bbar_re = k_re[..., None] * b_re - k_im[..., None] * b_im
    bbar_im = k_re[..., None] * b_im + k_im[..., None] * b_re

    nb, ng = N_CBLOCKS, GROUPS_PER_BLOCK

    def per_block(w):
        return w.reshape((nb, ng) + w.shape[1:])

    bbd = jnp.concatenate([_blockdiag(per_block(bbar_re).transpose(0, 1, 3, 2)),
                           _blockdiag(per_block(bbar_im).transpose(0, 1, 3, 2))], axis=-1).astype(BF16)
    cbd = jnp.concatenate([_blockdiag(per_block(c_re).transpose(0, 1, 3, 2)),
                           _blockdiag(per_block(-c_im).transpose(0, 1, 3, 2))], axis=-2).astype(BF16)
    d = d_skip.reshape(nb, 1, LANES)
    wg = jnp.concatenate([_blockdiag(per_block(w_glu[..., :SSM_GROUP])),
                          _blockdiag(per_block(w_glu[..., SSM_GROUP:]))], axis=-1).astype(BF16)
    bg = jnp.concatenate([b_glu[:, :SSM_GROUP].reshape(nb, 1, LANES),
                          b_glu[:, SSM_GROUP:].reshape(nb, 1, LANES)], axis=-1)

    ar = abar_re.reshape(nb, STATE_BLOCK)
    ai = abar_im.reshape(nb, STATE_BLOCK)
    pw_r, pw_i = [ar], [ai]
    for _ in range(SUBLANES - 1):
        pr, pi = pw_r[-1], pw_i[-1]
        pw_r.append(pr * ar - pi * ai)
        pw_i.append(pr * ai + pi * ar)
    rows = jnp.arange(SUBLANES)[None, :, None]
    tabs = []
    for k in (1, 2, 4):
        keep = rows >= k
        tabs.append(jnp.where(keep, pw_r[k - 1][:, None, :], 0.0))
        tabs.append(jnp.where(keep, pw_i[k - 1][:, None, :], 0.0))
    tabs.append(jnp.stack(pw_r, axis=1))
    tabs.append(jnp.stack(pw_i, axis=1))
    tab = jnp.stack(tabs, axis=1)
    return bbd, tab, cbd, d, wg, bg


def _out_proj_kernel(x_ref, a_ref, b_ref, c_ref, w_ref, o_ref):
    acc = jnp.dot(a_ref[...], w_ref[0:W_A, :], preferred_element_type=F32)
    acc += jnp.dot(b_ref[...], w_ref[W_A:W_A + W_B, :], preferred_element_type=F32)
    acc += jnp.dot(c_ref[...], w_ref[W_A + W_B:W_MIX, :], preferred_element_type=F32)
    o_ref[...] = x_ref[...] + acc


def out_proj(x, a, b, c, w, *, bm=1024, bn=1024):
    m, n = x.shape
    return pl.pallas_call(
        _out_proj_kernel,
        grid=(m // bm, n // bn),
        in_specs=[pl.BlockSpec((bm, bn), lambda i, j: (i, j)),
                  pl.BlockSpec((bm, W_A), lambda i, j: (i, 0)),
                  pl.BlockSpec((bm, W_B), lambda i, j: (i, 0)),
                  pl.BlockSpec((bm, W_C), lambda i, j: (i, 0)),
                  pl.BlockSpec((W_MIX, bn), lambda i, j: (0, j))],
        out_specs=pl.BlockSpec((bm, bn), lambda i, j: (i, j)),
        out_shape=jax.ShapeDtypeStruct((m, n), F32),
        compiler_params=_params(("parallel", "arbitrary")),
    )(x, a, b, c, w)


def _ffn_up_kernel(x_ref, g_ref, wg_ref, wu_ref, o_ref, h_ref, *, bm):
    @pl.when(pl.program_id(1) == 0)
    def _():
        _rmsnorm_rows_to(x_ref, g_ref, h_ref, bm)
    h = h_ref[...]
    gate = jnp.dot(h, wg_ref[...], preferred_element_type=F32)
    up = jnp.dot(h, wu_ref[...], preferred_element_type=F32)
    o_ref[...] = (gate * jax.nn.sigmoid(gate) * up).astype(BF16)


def ffn_up(x, g, w_gu, *, bm=1024, bn=512):
    m, k = x.shape
    nj = D_FF // bn
    return pl.pallas_call(
        functools.partial(_ffn_up_kernel, bm=bm),
        grid=(m // bm, nj),
        in_specs=[pl.BlockSpec((bm, k), lambda i, j: (i, 0)),
                  pl.BlockSpec((1, k), lambda i, j: (0, 0)),
                  pl.BlockSpec((k, bn), lambda i, j: (0, j)),
                  pl.BlockSpec((k, bn), lambda i, j: (0, nj + j))],
        out_specs=pl.BlockSpec((bm, bn), lambda i, j: (i, j)),
        out_shape=jax.ShapeDtypeStruct((m, D_FF), BF16),
        scratch_shapes=[pltpu.VMEM((bm, k), BF16)],
        compiler_params=_params(("parallel", "arbitrary")),
    )(x, g.reshape(1, k), w_gu, w_gu)


def _ffn_down_kernel(x_ref, a_ref, w_ref, o_ref):
    o_ref[...] = x_ref[...] + jnp.dot(a_ref[...], w_ref[...], preferred_element_type=F32)


def ffn_down(x, act, w, *, bm=512, bn=1024):
    m, n = x.shape
    k = act.shape[1]
    return pl.pallas_call(
        _ffn_down_kernel,
        grid=(m // bm, n // bn),
        in_specs=[pl.BlockSpec((bm, bn), lambda i, j: (i, j)),
                  pl.BlockSpec((bm, k), lambda i, j: (i, 0)),
                  pl.BlockSpec((k, bn), lambda i, j: (0, j))],
        out_specs=pl.BlockSpec((bm, bn), lambda i, j: (i, j)),
        out_shape=jax.ShapeDtypeStruct((m, n), F32),
        compiler_params=_params(("parallel", "arbitrary")),
    )(x, act, w)


def _final_norm_kernel(x_ref, g_ref, o_ref):
    x = x_ref[...]
    ms = jnp.mean(x * x, axis=-1, keepdims=True)
    o_ref[...] = x * lax.rsqrt(ms + EPS) * g_ref[...]


def final_norm(x, g, *, row0, nrows, bm=256):
    blk0 = row0 // bm
    k = x.shape[1]
    return pl.pallas_call(
        _final_norm_kernel,
        grid=(nrows // bm,),
        in_specs=[pl.BlockSpec((bm, k), lambda i: (blk0 + i, 0)),
                  pl.BlockSpec((1, k), lambda i: (0, 0))],
        out_specs=pl.BlockSpec((bm, k), lambda i: (i, 0)),
        out_shape=jax.ShapeDtypeStruct((nrows, k), F32),
        compiler_params=_params(("parallel",)),
    )(x, g.reshape(1, k))


def _gate_weights(w_s, b_s):
    bias_p = jnp.repeat(b_s.T, LANES, axis=1)
    corner = w_s[:, :DEC_SEQ, :DEC_SEQ]
    reps = CHUNK // DEC_SEQ
    w_smp = jnp.einsum("hrc,ab->harbc", corner, jnp.eye(reps, dtype=w_s.dtype)).reshape(
        N_HEADS_B, CHUNK, CHUNK)
    bias_s = jnp.tile(bias_p[:DEC_SEQ], (reps, 1))
    return bias_p, w_smp, bias_s


def kernel(x_prompt, x_sample, state_conv, state_ssm_re, state_ssm_im, g_mix, w_in, conv_w, conv_b,
           ln_a_g, ln_a_b, ln_v_g, ln_v_b, w_s, b_s, a_re, a_im, log_dt, b_re, b_im, c_re, c_im,
           d_skip, w_glu, b_glu, w_out, g_ffn, w_gu, w_down, g_final):
    x = jnp.concatenate([x_prompt.reshape(ROWS_P, D_MODEL), x_sample.reshape(ROWS_S, D_MODEL)], axis=0)
    w_in_b = w_in.astype(BF16)
    w_out_b = w_out.astype(BF16)
    w_gu_b = w_gu.astype(BF16)
    w_down_b = w_down.astype(BF16)

    conv_p, ssr_p, ssi_p, conv_s, ssr_s, ssi_s, v_s = [], [], [], [], [], [], []
    for l in range(DEPTH):
        proj = in_proj(x, g_mix[l], w_in_b[l])

        out_a, cp = conv_prompt(proj, conv_w[l], conv_b[l], ln_a_g[l], ln_a_b[l])
        out_a, cs = conv_sample(proj, state_conv[l], conv_w[l], conv_b[l], ln_a_g[l], ln_a_b[l], out_a)

        bias_p, w_smp, bias_s = _gate_weights(w_s[l], b_s[l])
        out_b = spatial_gate(proj, ln_v_g[l], ln_v_b[l], w_s[l], bias_p,
                             row0=0, nrows=ROWS_P, cpb=4, blk=CHUNK)[0]
        out_b, vn = spatial_gate(proj, ln_v_g[l], ln_v_b[l], w_smp, bias_s,
                                 row0=ROWS_P, nrows=ROWS_S, cpb=1, blk=DEC_SEQ, out_buf=out_b)

        wts = ssm_weights(a_re[l], a_im[l], log_dt[l], b_re[l], b_im[l], c_re[l], c_im[l],
                          d_skip[l], w_glu[l], b_glu[l])
        out_c, hrp, hip = ssm_prompt(proj, wts)
        out_c, hrs, his = ssm_sample(proj, wts,
                                     state_ssm_re[l].reshape(N_SAMPLE, -1),
                                     state_ssm_im[l].reshape(N_SAMPLE, -1), out_c)

        x = out_proj(x, out_a, out_b, out_c, w_out_b[l])
        act = ffn_up(x, g_ffn[l], w_gu_b[l])
        x = ffn_down(x, act, w_down_b[l])

        conv_p.append(cp)
        ssr_p.append(hrp.reshape(N_PROMPT, N_GROUPS_C, SSM_STATE))
        ssi_p.append(hip.reshape(N_PROMPT, N_GROUPS_C, SSM_STATE))
        conv_s.append(cs)
        ssr_s.append(hrs.reshape(N_SAMPLE, N_GROUPS_C, SSM_STATE))
        ssi_s.append(his.reshape(N_SAMPLE, N_GROUPS_C, SSM_STATE))
        v_s.append(vn.reshape(N_SAMPLE, DEC_SEQ, W_B))

    y_prompt = final_norm(x, g_final, row0=0, nrows=ROWS_P).reshape(N_PROMPT, SEQ, D_MODEL)
    y_sample = final_norm(x, g_final, row0=ROWS_P, nrows=ROWS_S).reshape(N_SAMPLE, DEC_SEQ, D_MODEL)
    return (y_prompt, y_sample, jnp.stack(conv_p), jnp.stack(ssr_p), jnp.stack(ssi_p),
            jnp.stack(conv_s), jnp.stack(ssr_s), jnp.stack(ssi_s), jnp.stack(v_s))
```
